```python
import functools
import jax, jax.numpy as jnp
from jax import lax
import numpy as np

D_MODEL = 1024
BATCH = 4
SEQ = 4096
DEPTH = 2
DEC_BATCH = 32
DEC_SEQ = 8
PAST_LEN = 8192
PAGE_SIZE = 128

CONV_DIM = D_MODEL // 4
CONV_GROUPS = 4
CONV_W = 3
HG_WIDTH = D_MODEL // 4
HG_HEADS = 4
HG_DK = HG_WIDTH // HG_HEADS
HG_DV = HG_WIDTH // HG_HEADS
HG_CHUNK = 64
ATT_WIDTH = D_MODEL // 2
HEAD_DIM = 64
N_HEADS = ATT_WIDTH // HEAD_DIM
N_KV = 4
KV_WIDTH = N_KV * HEAD_DIM
IDX_HEADS = 8
IDX_DIM = 32
TOPK_MAX = 256
Q_BLOCK = 128
MIX_WIDTH = CONV_DIM + HG_WIDTH + ATT_WIDTH
N_MEM = 256
X_HEADS = 4
X_HEAD_DIM = 128
X_WIDTH = X_HEADS * X_HEAD_DIM
D_FF = 4 * D_MODEL
ROPE_THETA = 10000.0
LN_EPS = 1e-5
DN_ALPHA = (2 * DEPTH) ** 0.25
DN_BETA = (8 * DEPTH) ** -0.25
SPLIT_SIZES = (CONV_DIM, CONV_DIM, CONV_DIM, HG_WIDTH, HG_WIDTH, HG_WIDTH, HG_WIDTH,
               ATT_WIDTH, KV_WIDTH, KV_WIDTH, IDX_HEADS * IDX_DIM, IDX_DIM, IDX_HEADS)
SPLIT_POINTS = tuple(int(s) for s in np.cumsum(SPLIT_SIZES)[:-1])
N_IN = sum(SPLIT_SIZES)

kernel_name = 'hymba_conv_hgrn2_dsa_deepnorm_step'

F32 = jnp.float32


def layer_norm(x, g, b):
    xf = x.astype(F32)
    mu = xf.mean(-1, keepdims=True)
    var = jnp.square(xf - mu).mean(-1, keepdims=True)
    return ((xf - mu) * lax.rsqrt(var + LN_EPS) * g.astype(F32) + b.astype(F32)).astype(x.dtype)


def rope(x, pos):
    half = x.shape[-1] // 2
    inv = ROPE_THETA ** (-jnp.arange(half, dtype=F32) / half)
    ang = pos.astype(F32)[:, None] * inv[None, :]
    cos, sin = jnp.cos(ang)[:, None, :], jnp.sin(ang)[:, None, :]
    x1, x2 = x[..., :half].astype(F32), x[..., half:].astype(F32)
    return jnp.concatenate([x1 * cos - x2 * sin, x2 * cos + x1 * sin], axis=-1).astype(x.dtype)


def short_gated_conv(h, bg, cg, w_conv, buf):
    u = cg * h
    ext = jnp.concatenate([buf.astype(u.dtype), u], axis=1)
    L = u.shape[1]
    y = sum(w_conv[j] * ext[:, j:j + L] for j in range(CONV_W))
    return bg * y, ext[:, ext.shape[1] - (CONV_W - 1):]


def hgrn_lower_bounds(logits):
    p = jax.nn.softmax(logits.astype(F32), axis=0)
    c = jnp.cumsum(p, axis=0)
    return c - c[0:1]


def hgrn2_chunked(q, k, v, logf, s0):
    B, L, H, DK = q.shape
    C = HG_CHUNK if L % HG_CHUNK == 0 else L
    n = L // C
    def chunks(a):
        return a.reshape(B, n, C, H, a.shape[-1]).transpose(1, 0, 3, 2, 4).astype(F32)
    causal = jnp.tril(jnp.ones((C, C), bool))[:, :, None]
    def step(S, inp):
        qi, ki, vi, gi = inp
        b = jnp.cumsum(gi, axis=2)
        inter = jnp.einsum('bhtk,bhkv->bhtv', qi * jnp.exp(b), S)
        diff = b[:, :, :, None, :] - b[:, :, None, :, :]
        decay = jnp.exp(jnp.where(causal, diff, -jnp.inf))
        att = jnp.einsum('bhtk,bhtsk,bhsk->bhts', qi, decay, ki)
        intra = jnp.einsum('bhts,bhsv->bhtv', att, vi)
        bC = b[:, :, -1:, :]
        S_new = jnp.exp(bC[:, :, 0, :])[..., None] * S + jnp.einsum('bhsk,bhsv->bhkv', ki * jnp.exp(bC - b), vi)
        return S_new, inter + intra
    S, o = lax.scan(step, s0.astype(F32), (chunks(q), chunks(k), chunks(v), chunks(logf)))
    o = o.transpose(1, 0, 3, 2, 4).reshape(B, L, H, v.shape[-1])
    return o, S


def index_scores(qi, ki, wi):
    dots = jnp.einsum('bthd,bsd->bths', qi.astype(F32), ki.astype(F32))
    return jnp.einsum('bth,bths->bts', wi.astype(F32), jax.nn.relu(dots))


def sparse_attend(q, kg, vg, ok):
    B, T = q.shape[:2]
    qg = q.reshape(B, T, N_KV, N_HEADS // N_KV, HEAD_DIM).astype(F32)
    s = jnp.einsum('btngd,btknd->btngk', qg, kg.astype(F32)) * (HEAD_DIM ** -0.5)
    s = jnp.where(ok[:, :, None, None, :], s, -jnp.inf)
    p = jax.nn.softmax(s, axis=-1)
    o = jnp.einsum('btngk,btknd->btngd', p, vg.astype(F32))
    return o.reshape(B, T, N_HEADS, HEAD_DIM).astype(q.dtype)


def take_rows(a, idx):
    return jax.vmap(lambda aa, ii: aa[ii])(a, idx)


def dsa_prompt(q, k, v, qi, ki, wi):
    B, T = q.shape[:2]
    ksel = min(TOPK_MAX, T // 4)
    key_pos = jnp.arange(T)
    def block(i):
        t0 = i * Q_BLOCK
        sl = lambda a: lax.dynamic_slice_in_dim(a, t0, Q_BLOCK, axis=1)
        qpos = t0 + jnp.arange(Q_BLOCK)
        score = index_scores(sl(qi), ki, sl(wi))
        score = jnp.where(key_pos[None, None, :] <= qpos[None, :, None], score, -jnp.inf)
        _, idx = lax.top_k(score, ksel)
        ok = idx <= qpos[None, :, None]
        return sparse_attend(sl(q), take_rows(k, idx), take_rows(v, idx), ok)
    out = lax.map(block, jnp.arange(T // Q_BLOCK))
    return jnp.moveaxis(out, 0, 1).reshape(B, T, N_HEADS, HEAD_DIM)


def dsa_sample(q, k, v, qi, ki, wi, cache_k, cache_v, cache_kidx, page_table, layer):
    DB, T = q.shape[:2]
    page = cache_k.shape[2]
    past = page_table.shape[1] * page
    L = past + T
    ksel = min(TOPK_MAX, L // 4)
    ki_past = cache_kidx[layer, page_table].reshape(DB, past, IDX_DIM)
    ki_all = jnp.concatenate([ki_past.astype(ki.dtype), ki], axis=1)
    qpos = past + jnp.arange(T)
    score = index_scores(qi, ki_all, wi)
    score = jnp.where(jnp.arange(L)[None, None, :] <= qpos[None, :, None], score, -jnp.inf)
    _, idx = lax.top_k(score, ksel)
    ok = idx <= qpos[None, :, None]
    is_new = (idx >= past)[..., None, None]
    pidx = jnp.minimum(idx, past - 1)
    phys = jax.vmap(lambda pt, j: pt[j])(page_table, pidx // page)
    off = pidx % page
    nidx = jnp.clip(idx - past, 0, T - 1)
    kg = jnp.where(is_new, take_rows(k, nidx), cache_k[layer, phys, off].astype(k.dtype))
    vg = jnp.where(is_new, take_rows(v, nidx), cache_v[layer, phys, off].astype(v.dtype))
    return sparse_attend(q, kg, vg, ok)


def token_mixer(x, pos, w_in_l, w_conv_l, lb_l, hg_norm_l, w_out_l, conv_buf, hg_s0, attend):
    B, L, _ = x.shape
    z = x @ w_in_l
    (c_h, c_b, c_c, g_q, g_f, g_i, g_g, a_q, a_k, a_v, a_qi, a_ki, a_w) = jnp.split(z, SPLIT_POINTS, axis=-1)
    y_conv, conv_state = short_gated_conv(c_h, c_b, c_c, w_conv_l, conv_buf)
    sig = jax.nn.sigmoid(g_f.astype(F32))
    f = lb_l + (1.0 - lb_l) * sig
    kk = (1.0 - lb_l) * (1.0 - sig)
    heads = lambda a: a.reshape(B, L, HG_HEADS, a.shape[-1] // HG_HEADS)
    o_hg, hg_state = hgrn2_chunked(heads(g_q), heads(kk), heads(g_i), heads(jnp.log(f)), hg_s0)
    o_hg = o_hg * lax.rsqrt(jnp.mean(jnp.square(o_hg), -1, keepdims=True) + LN_EPS)
    o_hg = o_hg * hg_norm_l.astype(F32).reshape(HG_HEADS, HG_DV)
    y_hg = (o_hg.reshape(B, L, HG_WIDTH) * jax.nn.sigmoid(g_g.astype(F32))).astype(x.dtype)
    q = rope(a_q.reshape(B, L, N_HEADS, HEAD_DIM), pos)
    k = rope(a_k.reshape(B, L, N_KV, HEAD_DIM), pos)
    v = a_v.reshape(B, L, N_KV, HEAD_DIM)
    qi = rope(a_qi.reshape(B, L, IDX_HEADS, IDX_DIM), pos) * (IDX_DIM ** -0.5)
    ki = rope(a_ki[:, :, None, :], pos)[:, :, 0, :]
    wi = a_w * (IDX_HEADS ** -0.5)
    y_att = attend(q, k, v, qi, ki, wi).reshape(B, L, ATT_WIDTH)
    y = jnp.concatenate([y_conv, y_hg, y_att.astype(x.dtype)], axis=-1) @ w_out_l
    return y, conv_state, hg_state, k, v, ki


def cross_and_ffn(h, mk, mv, w_xq_l, w_xo_l, g2, b2, w_up_l, w_down_l, g3, b3):
    B, T, _ = h.shape
    q = (h @ w_xq_l).reshape(B, T, X_HEADS, X_HEAD_DIM)
    s = jnp.einsum('bthd,bmhd->bhtm', q.astype(F32), mk.astype(F32)) * (X_HEAD_DIM ** -0.5)
    p = jax.nn.softmax(s, axis=-1)
    o = jnp.einsum('bhtm,bmhd->bthd', p, mv.astype(F32)).reshape(B, T, X_WIDTH).astype(h.dtype)
    h = layer_norm(DN_ALPHA * h + o @ w_xo_l, g2, b2)
    u = jax.nn.relu(h @ w_up_l)
    return layer_norm(DN_ALPHA * h + (u * u) @ w_down_l, g3, b3)


def setup_inputs(seed: int = 0) -> dict:
    key = jax.random.key(seed)
    ks = iter(jax.random.split(key, 40))
    nrm = lambda shape, scale=1.0: jax.random.normal(next(ks), shape, F32) * scale
    n_pages = PAST_LEN // PAGE_SIZE
    n_phys = (DEC_BATCH * n_pages * 5) // 4
    page_table = jax.random.permutation(next(ks), n_phys)[:DEC_BATCH * n_pages]
    page_table = page_table.reshape(DEC_BATCH, n_pages).astype(jnp.int32)
    return {
        'x_prompt': nrm((BATCH, SEQ, D_MODEL)),
        'x_sample': nrm((DEC_BATCH, DEC_SEQ, D_MODEL)),
        'cache_k': nrm((DEPTH, n_phys, PAGE_SIZE, N_KV, HEAD_DIM)),
        'cache_v': nrm((DEPTH, n_phys, PAGE_SIZE, N_KV, HEAD_DIM)),
        'cache_kidx': nrm((DEPTH, n_phys, PAGE_SIZE, IDX_DIM)),
        'state_hgrn': nrm((DEPTH, DEC_BATCH, HG_HEADS, HG_DK, HG_DV), 0.5),
        'state_conv': nrm((DEPTH, DEC_BATCH, CONV_W - 1, CONV_DIM)),
        'cache_mem_k': nrm((DEPTH, DEC_BATCH, N_MEM, X_HEADS, X_HEAD_DIM)),
        'cache_mem_v': nrm((DEPTH, DEC_BATCH, N_MEM, X_HEADS, X_HEAD_DIM)),
        'page_table': page_table,
        'mem_prompt': nrm((BATCH, N_MEM, D_MODEL)),
        'w_in': nrm((DEPTH, D_MODEL, N_IN), D_MODEL ** -0.5),
        'w_conv': nrm((DEPTH, CONV_W, CONV_DIM), CONV_W ** -0.5),
        'hg_lb_logits': nrm((DEPTH, HG_WIDTH), 0.5),
        'hg_norm': 1.0 + nrm((DEPTH, HG_WIDTH), 0.02),
        'w_out': nrm((DEPTH, MIX_WIDTH, D_MODEL), DN_BETA * MIX_WIDTH ** -0.5),
        'ln1_g': 1.0 + nrm((DEPTH, D_MODEL), 0.02),
        'ln1_b': nrm((DEPTH, D_MODEL), 0.02),
        'w_xq': nrm((DEPTH, D_MODEL, X_WIDTH), D_MODEL ** -0.5),
        'w_xk': nrm((DEPTH, D_MODEL, X_WIDTH), D_MODEL ** -0.5),
        'w_xv': nrm((DEPTH, D_MODEL, X_WIDTH), D_MODEL ** -0.5),
        'w_xo': nrm((DEPTH, X_WIDTH, D_MODEL), DN_BETA * X_WIDTH ** -0.5),
        'ln2_g': 1.0 + nrm((DEPTH, D_MODEL), 0.02),
        'ln2_b': nrm((DEPTH, D_MODEL), 0.02),
        'w_up': nrm((DEPTH, D_MODEL, D_FF), D_MODEL ** -0.5),
        'w_down': nrm((DEPTH, D_FF, D_MODEL), DN_BETA * D_FF ** -0.5),
        'ln3_g': 1.0 + nrm((DEPTH, D_MODEL), 0.02),
        'ln3_b': nrm((DEPTH, D_MODEL), 0.02),
    }


def reference(x_prompt, x_sample, cache_k, cache_v, cache_kidx, state_hgrn, state_conv, cache_mem_k, cache_mem_v,
              page_table, mem_prompt, w_in, w_conv, hg_lb_logits, hg_norm, w_out, ln1_g, ln1_b,
              w_xq, w_xk, w_xv, w_xo, ln2_g, ln2_b, w_up, w_down, ln3_g, ln3_b):
    Bp, Tp, _ = x_prompt.shape
    past = page_table.shape[1] * cache_k.shape[2]
    pos_p = jnp.arange(Tp)
    pos_s = past + jnp.arange(x_sample.shape[1])
    lower = hgrn_lower_bounds(hg_lb_logits)
    hp, hs = x_prompt, x_sample
    kp, vp, kip, ksl, vsl, kis, hgp, hgs, cvp, cvs, mkp, mvp = ([] for _ in range(12))
    for l in range(DEPTH):
        mixer_w = (w_in[l], w_conv[l], lower[l], hg_norm[l], w_out[l])
        tail_w = (w_xq[l], w_xo[l], ln2_g[l], ln2_b[l], w_up[l], w_down[l], ln3_g[l], ln3_b[l])
        conv0 = jnp.zeros((Bp, CONV_W - 1, CONV_DIM), hp.dtype)
        hg0 = jnp.zeros((Bp, HG_HEADS, HG_DK, HG_DV), F32)
        y, c_st, h_st, k, v, ki = token_mixer(hp, pos_p, *mixer_w, conv0, hg0, dsa_prompt)
        hp = layer_norm(DN_ALPHA * hp + y, ln1_g[l], ln1_b[l])
        mk = (mem_prompt @ w_xk[l]).reshape(Bp, N_MEM, X_HEADS, X_HEAD_DIM)
        mv = (mem_prompt @ w_xv[l]).reshape(Bp, N_MEM, X_HEADS, X_HEAD_DIM)
        hp = cross_and_ffn(hp, mk, mv, *tail_w)
        kp.append(k); vp.append(v); kip.append(ki); hgp.append(h_st); cvp.append(c_st); mkp.append(mk); mvp.append(mv)
        attend = functools.partial(dsa_sample, cache_k=cache_k, cache_v=cache_v, cache_kidx=cache_kidx,
                                   page_table=page_table, layer=l)
        y, c_st, h_st, k, v, ki = token_mixer(hs, pos_s, *mixer_w, state_conv[l], state_hgrn[l], attend)
        hs = layer_norm(DN_ALPHA * hs + y, ln1_g[l], ln1_b[l])
        hs = cross_and_ffn(hs, cache_mem_k[l], cache_mem_v[l], *tail_w)
        ksl.append(k); vsl.append(v); kis.append(ki); hgs.append(h_st); cvs.append(c_st)
    st = lambda xs: jnp.stack(xs, axis=0)
    return (hp, hs, st(kp), st(vp), st(kip), st(ksl), st(vsl), st(kis),
            st(hgp), st(hgs), st(cvp), st(cvs), st(mkp), st(mvp))
```

```python
import functools

import numpy as np
import jax
import jax.numpy as jnp
from jax import lax
from jax.experimental import pallas as pl
from jax.experimental.pallas import tpu as pltpu

F32 = jnp.float32
BF16 = jnp.bfloat16
I32 = jnp.int32

HG_HEADS = 4
HEAD_DIM = 64
N_KV = 4
IDX_HEADS = 8
IDX_DIM = 32
TOPK_MAX = 256
X_HEADS = 4
CONV_W = 3
ROPE_THETA = 10000.0
LN_EPS = 1e-5

LANES = 128
VMEM_LIMIT = 56 * 1024 * 1024
INT_MIN = -2 ** 31
NEG_BIG = -1e30


def _cparams(*sem):
    return pltpu.CompilerParams(dimension_semantics=sem, vmem_limit_bytes=VMEM_LIMIT)


def _mm_kernel(x_ref, w_ref, o_ref, *, act):
    acc = jnp.dot(x_ref[...].astype(BF16), w_ref[...], preferred_element_type=F32)
    if act == "relu2":
        r = jnp.maximum(acc, 0.0)
        acc = r * r
    o_ref[...] = acc.astype(o_ref.dtype)


def matmul(x, w, *, act=None, out_dtype=F32, tm=512, tn=None):
    M, K = x.shape
    N = w.shape[1]
    tm = min(tm, M)
    if tn is None:
        tn = next(c for c in range(min(N, 1024), 0, -LANES) if N % c == 0)
    assert M % tm == 0 and N % tn == 0
    return pl.pallas_call(
        functools.partial(_mm_kernel, act=act),
        grid=(M // tm, N // tn),
        in_specs=[pl.BlockSpec((tm, K), lambda i, j: (i, 0)),
                  pl.BlockSpec((K, tn), lambda i, j: (0, j))],
        out_specs=pl.BlockSpec((tm, tn), lambda i, j: (i, j)),
        out_shape=jax.ShapeDtypeStruct((M, N), out_dtype),
        compiler_params=_cparams("parallel", "arbitrary"),
        name="matmul",
    )(x, w)


def _mm_ln_kernel(x_ref, w_ref, r_ref, g_ref, b_ref, o_ref, *, alpha):
    acc = jnp.dot(x_ref[...].astype(BF16), w_ref[...], preferred_element_type=F32)
    h = alpha * r_ref[...] + acc
    mu = jnp.mean(h, axis=-1, keepdims=True)
    d = h - mu
    var = jnp.mean(d * d, axis=-1, keepdims=True)
    o_ref[...] = d * lax.rsqrt(var + LN_EPS) * g_ref[...] + b_ref[...]


def matmul_res_ln(x, w, res, g, b, *, alpha, tm=256):
    M, K = x.shape
    N = w.shape[1]
    tm = min(tm, M)
    assert M % tm == 0
    return pl.pallas_call(
        functools.partial(_mm_ln_kernel, alpha=alpha),
        grid=(M // tm,),
        in_specs=[pl.BlockSpec((tm, K), lambda i: (i, 0)),
                  pl.BlockSpec((K, N), lambda i: (0, 0)),
                  pl.BlockSpec((tm, N), lambda i: (i, 0)),
                  pl.BlockSpec((1, N), lambda i: (0, 0)),
                  pl.BlockSpec((1, N), lambda i: (0, 0))],
        out_specs=pl.BlockSpec((tm, N), lambda i: (i, 0)),
        out_shape=jax.ShapeDtypeStruct((M, N), F32),
        compiler_params=_cparams("parallel"),
        name="matmul_res_ln",
    )(x, w, res, g.reshape(1, N), b.reshape(1, N))


def _xattn_kernel(q_ref, mk_ref, mv_ref, o_ref, *, hd):
    scale = hd ** -0.5
    for h in range(X_HEADS):
        sl = slice(h * hd, (h + 1) * hd)
        q = q_ref[0, :, sl].astype(BF16)
        k = mk_ref[0, :, sl].astype(BF16)
        v = mv_ref[0, :, sl].astype(BF16)
        s = lax.dot_general(q, k, (((1,), (1,)), ((), ())), preferred_element_type=F32) * scale
        m = jnp.max(s, axis=-1, keepdims=True)
        p = jnp.exp(s - m)
        l = jnp.sum(p, axis=-1, keepdims=True)
        o = jnp.dot(p.astype(BF16), v, preferred_element_type=F32)
        o_ref[0, :, sl] = o / l


def cross_attention(q, mk, mv, *, tq=512):
    B, T, XW = q.shape
    NM = mk.shape[1]
    tq = min(tq, T)
    assert T % tq == 0
    return pl.pallas_call(
        functools.partial(_xattn_kernel, hd=XW // X_HEADS),
        grid=(B, T // tq),
        in_specs=[pl.BlockSpec((1, tq, XW), lambda b, i: (b, i, 0)),
                  pl.BlockSpec((1, NM, XW), lambda b, i: (b, 0, 0)),
                  pl.BlockSpec((1, NM, XW), lambda b, i: (b, 0, 0))],
        out_specs=pl.BlockSpec((1, tq, XW), lambda b, i: (b, i, 0)),
        out_shape=jax.ShapeDtypeStruct((B, T, XW), F32),
        compiler_params=_cparams("parallel", "arbitrary"),
        name="cross_attention",
    )(q, mk, mv)


def _split3(x):
    a = x.astype(BF16)
    r = x - a.astype(F32)
    b = r.astype(BF16)
    c = (r - b.astype(F32)).astype(BF16)
    return a, b, c


def _hgrn_kernel(gq_ref, gf_ref, gi_ref, gg_ref, lb_ref, nrm_ref, s0_ref, y_ref, st_ref, s_scr,
                 *, TB, CH, SB, HW):
    dk = HW // HG_HEADS
    c = pl.program_id(1)

    @pl.when(c == 0)
    def _():
        s_scr[...] = s0_ref[0]

    lb = lb_ref[...]
    lane_head = lax.broadcasted_iota(I32, (1, HW), 1) // dk
    tril = (lax.broadcasted_iota(I32, (CH, CH), 0) >= lax.broadcasted_iota(I32, (CH, CH), 1)
            ).astype(BF16)
    blk = (lax.broadcasted_iota(I32, (HW, HW), 0) // dk == lax.broadcasted_iota(I32, (HW, HW), 1) // dk)
    gmean = jnp.where(blk, 1.0 / dk, 0.0).astype(BF16)
    row_ch = lax.broadcasted_iota(I32, (CH, 1), 0)
    NB = CH // SB
    S = s_scr[...]
    for cc in range(TB // CH):
        sl = slice(cc * CH, (cc + 1) * CH)
        gq = gq_ref[0, sl, :]
        gi = gi_ref[0, sl, :]
        sig = jax.nn.sigmoid(gf_ref[0, sl, :])
        f = lb + (1.0 - lb) * sig
        kk = (1.0 - lb) * (1.0 - sig)
        lf = jnp.log(f)
        l1, l2, l3 = _split3(lf)
        b = (jnp.dot(tril, l1, preferred_element_type=F32)
             + jnp.dot(tril, l2, preferred_element_type=F32)
             + jnp.dot(tril, l3, preferred_element_type=F32))
        gi_b = gi.astype(BF16)
        qb = (gq * jnp.exp(b)).astype(BF16)
        o_parts = lax.dot_general(qb, S.astype(BF16), (((1,), (1,)), ((), ())),
                                  preferred_element_type=F32)
        outs = []
        for i in range(NB):
            rows = slice(i * SB, (i + 1) * SB)
            nk = (i + 1) * SB
            m_i = b[i * SB + SB // 2 - 1: i * SB + SB // 2, :]
            qh = gq[rows] * jnp.exp(b[rows] - m_i)
            qi = jnp.concatenate([jnp.where(lane_head == h, qh, 0.0) for h in range(HG_HEADS)],
                                 axis=0).astype(BF16)
            ki = (kk[:nk] * jnp.exp(m_i - b[:nk])).astype(BF16)
            att = lax.dot_general(qi, ki, (((1,), (1,)), ((), ())),
                                  preferred_element_type=F32)
            tq = lax.broadcasted_iota(I32, (HG_HEADS * SB, nk), 0) % SB + i * SB
            ts = lax.broadcasted_iota(I32, (HG_HEADS * SB, nk), 1)
            att = jnp.where(ts <= tq, att, 0.0).astype(BF16)
            oi = jnp.dot(att, gi_b[:nk], preferred_element_type=F32)
            acc = jnp.where(lane_head == 0, oi[0:SB], 0.0)
            for h in range(1, HG_HEADS):
                acc = acc + jnp.where(lane_head == h, oi[h * SB:(h + 1) * SB], 0.0)
            outs.append(acc)
        o = o_parts + (jnp.concatenate(outs, axis=0) if NB > 1 else outs[0])
        bC = b[CH - 1:CH, :]
        kdec = (kk * jnp.exp(bC - b)).astype(BF16)
        upd = lax.dot_general(gi_b, kdec, (((0,), (0,)), ((), ())),
                              preferred_element_type=F32)
        S = jnp.where(blk, S * jnp.exp(bC) + upd, 0.0)
        o2 = o * o
        o2a = o2.astype(BF16)
        o2b = (o2 - o2a.astype(F32)).astype(BF16)
        ms = (jnp.dot(o2a, gmean, preferred_element_type=F32)
              + jnp.dot(o2b, gmean, preferred_element_type=F32))
        y = o * lax.rsqrt(ms + LN_EPS) * nrm_ref[...] * jax.nn.sigmoid(gg_ref[0, sl, :])
        y_ref[0, sl, :] = y
    s_scr[...] = S

    @pl.when(c == pl.num_programs(1) - 1)
    def _():
        st_ref[0] = S


def hgrn2(z, col0, lb, nrm, s0):
    B, L, _ = z.shape
    HW = lb.shape[-1]
    dk = HW // HG_HEADS
    assert col0 % HW == 0
    cb = col0 // HW
    if L % 64 == 0:
        CH, SB = 64, 16
        TB = 256 if L % 256 == 0 else 64
    else:
        assert L % 8 == 0 and L <= 64
        CH, SB, TB = L, 8, L
    eye = jnp.eye(HG_HEADS, dtype=F32)
    s0t = jnp.einsum("bhkv,hg->bgvhk", s0.astype(F32), eye).reshape(B, HW, HW)
    zspec = lambda k: pl.BlockSpec((1, TB, HW), lambda b, c: (b, c, cb + k))
    y, st = pl.pallas_call(
        functools.partial(_hgrn_kernel, TB=TB, CH=CH, SB=SB, HW=HW),
        grid=(B, L // TB),
        in_specs=[zspec(0), zspec(1), zspec(2), zspec(3),
                  pl.BlockSpec((1, HW), lambda b, c: (0, 0)),
                  pl.BlockSpec((1, HW), lambda b, c: (0, 0)),
                  pl.BlockSpec((1, HW, HW), lambda b, c: (b, 0, 0))],
        out_specs=[pl.BlockSpec((1, TB, HW), lambda b, c: (b, c, 0)),
                   pl.BlockSpec((1, HW, HW), lambda b, c: (b, 0, 0))],
        out_shape=[jax.ShapeDtypeStruct((B, L, HW), F32),
                   jax.ShapeDtypeStruct((B, HW, HW), F32)],
        scratch_shapes=[pltpu.VMEM((HW, HW), F32)],
        compiler_params=_cparams("parallel", "arbitrary"),
        name="hgrn2",
    )(z, z, z, z, lb.reshape(1, HW), nrm.reshape(1, HW), s0t)
    st = st.reshape(B, HG_HEADS, dk, HG_HEADS, dk)
    state = jnp.einsum("bgvhk,hg->bhkv", st, eye)
    return y, state


def _sort_key(score):
    bits = pltpu.bitcast(score, I32)
    return jnp.where(bits < 0, bits ^ jnp.int32(0x7FFFFFFF), bits)


def _kth_largest(count_ge, ksel, shape, total):
    def body(i, carry):
        thr, cnt_thr = carry
        cand = thr + jnp.left_shift(jnp.int32(1), 31 - i)
        cnt = count_ge(cand)
        ok = cnt >= ksel
        return jnp.where(ok, cand, thr), jnp.where(ok, cnt, cnt_thr)
    return lax.fori_loop(0, 32, body, (jnp.full(shape, INT_MIN, I32), jnp.full(shape, total, I32)))


def _tie_cut(count_eq_le, need, nbits, shape):
    def body(i, p):
        cand = p + jnp.left_shift(jnp.int32(1), nbits - 1 - i)
        return jnp.where(count_eq_le(cand - 1) < need, cand, p)
    return lax.fori_loop(0, nbits, body, jnp.zeros(shape, I32))


def _dsa_prompt_kernel(kia_ref, qit_ref, w_ref, k_ref, qt_ref, vt_ref, o_ref, keys_ref, bias_ref, p_ref,
                       *, QB, TS, KSEL, NBITS):
    qb = pl.program_id(1)
    n_t = ((qb + 1) * QB + TS - 1) // TS
    t_idx = qb * QB + lax.broadcasted_iota(I32, (1, QB), 1)
    qit = qit_ref[0, 0]
    wrow = w_ref[0, 0]

    def row0(j):
        return pl.multiple_of(j * TS, TS)

    def s_index(j):
        return j * TS + lax.broadcasted_iota(I32, (TS, QB), 0)

    def score_tile(j, _):
        ka = kia_ref[0, pl.ds(row0(j), TS), :]
        d = jnp.dot(ka, qit, preferred_element_type=F32)
        r = jnp.maximum(d, 0.0) * wrow
        sc = r[:, 0:QB]
        for h in range(1, IDX_HEADS):
            sc = sc + r[:, h * QB:(h + 1) * QB]
        key = jnp.where(s_index(j) <= t_idx, _sort_key(sc), INT_MIN)
        keys_ref[pl.ds(row0(j), TS), :] = key
        return 0

    lax.fori_loop(0, n_t, score_tile, 0)

    def count(pred):
        def body(j, acc):
            m = pred(keys_ref[pl.ds(row0(j), TS), :], j).astype(I32)
            return acc + jnp.sum(m.reshape(TS // 8, 8, QB), axis=0)
        acc = lax.fori_loop(0, n_t, body, jnp.zeros((8, QB), I32))
        return jnp.sum(acc, axis=0, keepdims=True)

    thr, cnt_thr = _kth_largest(lambda cand: count(lambda key, j: key >= cand), KSEL, (1, QB), n_t * TS)
    live = thr > INT_MIN
    p_ref[...] = jnp.where(live, jnp.int32(2 ** 30), -1)
    excess = jnp.max(jnp.where(live, cnt_thr - KSEL, 0))

    @pl.when(excess > 0)
    def _():
        need = KSEL - count(lambda key, j: key > thr)
        cut = _tie_cut(lambda p: count(lambda key, j: (key == thr) & (s_index(j) <= p)), need, NBITS, (1, QB))
        p_ref[...] = jnp.where(live, cut, -1)

    cut = p_ref[...]

    def bias_tile(j, _):
        key = keys_ref[pl.ds(row0(j), TS), :]
        sel = (key > thr) | ((key == thr) & (s_index(j) <= cut))
        bias_ref[pl.ds(row0(j), TS), :] = jnp.where(sel, 0.0, NEG_BIG)
        return 0

    lax.fori_loop(0, n_t, bias_tile, 0)

    for n in range(N_KV):
        qtn = qt_ref[0, 0, n]
        G = qtn.shape[1] // QB

        def att_tile(j, carry):
            m, l, acc = carry
            kt = k_ref[0, n, pl.ds(row0(j), TS), :]
            s = jnp.dot(kt, qtn, preferred_element_type=F32)
            bias = bias_ref[pl.ds(row0(j), TS), :]
            s = s + jnp.concatenate([bias] * G, axis=1)
            m_new = jnp.maximum(m, jnp.max(s, axis=0, keepdims=True))
            p = jnp.exp(s - m_new)
            corr = jnp.exp(m - m_new)
            l = l * corr + jnp.sum(p, axis=0, keepdims=True)
            vt = vt_ref[0, n, :, pl.ds(row0(j), TS)]
            acc = acc * corr + jnp.dot(vt, p.astype(BF16), preferred_element_type=F32)
            return m_new, l, acc

        hd = qtn.shape[0]
        m, l, acc = lax.fori_loop(
            0, n_t, att_tile,
            (jnp.full((1, G * QB), NEG_BIG, F32), jnp.zeros((1, G * QB), F32), jnp.zeros((hd, G * QB), F32)))
        o_ref[0, 0, n] = acc / l


def _hi_lo(x):
    hi = x.astype(BF16)
    lo = (x - hi.astype(F32)).astype(BF16)
    return hi, lo


def dsa_prompt(q, k, v, qi, ki, wi):
    B, T, H, HD = q.shape
    G = H // N_KV
    QB = 128
    TS = min(512, T)
    assert T % QB == 0 and T % TS == 0
    nQ = T // QB
    KSEL = min(TOPK_MAX, T // 4)
    assert TS >= KSEL
    DI3 = 3 * IDX_DIM
    qh, ql = _hi_lo(qi)
    kh, kl = _hi_lo(ki)
    qi3 = jnp.concatenate([qh, qh, ql], axis=-1)
    kia = jnp.concatenate([kh, kl, kh], axis=-1)
    qit = qi3.reshape(B, nQ, QB, IDX_HEADS, DI3).transpose(0, 1, 4, 3, 2).reshape(B, nQ, DI3, IDX_HEADS * QB)
    wrow = wi.reshape(B, nQ, QB, IDX_HEADS).transpose(0, 1, 3, 2).reshape(B, nQ, 1, IDX_HEADS * QB)
    qs = (q * (HD ** -0.5)).astype(BF16)
    qt = qs.reshape(B, nQ, QB, N_KV, G, HD).transpose(0, 1, 3, 5, 4, 2).reshape(B, nQ, N_KV, HD, G * QB)
    kb = k.astype(BF16).transpose(0, 2, 1, 3)
    vt = v.astype(BF16).transpose(0, 2, 3, 1)
    o = pl.pallas_call(
        functools.partial(_dsa_prompt_kernel, QB=QB, TS=TS, KSEL=KSEL, NBITS=max(1, (T - 1).bit_length())),
        grid=(B, nQ),
        in_specs=[pl.BlockSpec((1, T, DI3), lambda b, i: (b, 0, 0)),
                  pl.BlockSpec((1, 1, DI3, IDX_HEADS * QB), lambda b, i: (b, i, 0, 0)),
                  pl.BlockSpec((1, 1, 1, IDX_HEADS * QB), lambda b, i: (b, i, 0, 0)),
                  pl.BlockSpec((1, N_KV, T, HD), lambda b, i: (b, 0, 0, 0)),
                  pl.BlockSpec((1, 1, N_KV, HD, G * QB), lambda b, i: (b, i, 0, 0, 0)),
                  pl.BlockSpec((1, N_KV, HD, T), lambda b, i: (b, 0, 0, 0))],
        out_specs=pl.BlockSpec((1, 1, N_KV, HD, G * QB), lambda b, i: (b, i, 0, 0, 0)),
        out_shape=jax.ShapeDtypeStruct((B, nQ, N_KV, HD, G * QB), F32),
        scratch_shapes=[pltpu.VMEM((T, QB), I32), pltpu.VMEM((T, QB), F32), pltpu.VMEM((1, QB), I32)],
        compiler_params=_cparams("parallel", "arbitrary"),
        name="dsa_prompt",
    )(kia, qit, wrow, kb, qt, vt)
    o = o.reshape(B, nQ, N_KV, HD, G, QB).transpose(0, 1, 5, 2, 4, 3)
    return o.reshape(B, T, H * HD)


PAGES_PER_STEP = 8


def _dsa_sample_select_kernel(pt_ref, qh_ref, ql_ref, w_ref, kin_ref, *rest, T, PAGE, KSEL, NBITS):
    kx_refs = rest[:PAGES_PER_STEP]
    bias_ref, keys_ref = rest[PAGES_PER_STEP:]
    j = pl.program_id(1)
    nj = pl.num_programs(1)
    W = PAGES_PER_STEP * PAGE
    LP = keys_ref.shape[1]
    past = LP - PAGE
    qh = qh_ref[0]
    ql = ql_ref[0]
    w = w_ref[0]
    dn = (((1,), (1,)), ((), ()))

    def scores(d):
        r = jnp.maximum(d, 0.0) * w
        sc = r[0:T]
        for h in range(1, IDX_HEADS):
            sc = sc + r[h * T:(h + 1) * T]
        return sc

    kx = jnp.concatenate([r[0, 0] for r in kx_refs], axis=0)
    kh, kl = _hi_lo(kx)
    d = (lax.dot_general(qh, kh, dn, preferred_element_type=F32)
         + lax.dot_general(qh, kl, dn, preferred_element_type=F32)
         + lax.dot_general(ql, kh, dn, preferred_element_type=F32))
    keys_ref[:, pl.ds(pl.multiple_of(j * W, W), W)] = _sort_key(scores(d))

    @pl.when(j == nj - 1)
    def _():
        kn = kin_ref[0]
        q3 = jnp.concatenate([qh, qh, ql], axis=1)
        dnew = lax.dot_general(q3, kn, dn, preferred_element_type=F32)
        jn = lax.broadcasted_iota(I32, (T, PAGE), 1)
        tq = lax.broadcasted_iota(I32, (T, PAGE), 0)
        keys_ref[:, past:] = jnp.where(jn <= tq, _sort_key(scores(dnew)), INT_MIN)
        keys = keys_ref[...]
        pos = lax.broadcasted_iota(I32, (T, LP), 1)

        def count(m):
            return jnp.sum(m.astype(I32), axis=1, keepdims=True)

        thr, cnt_thr = _kth_largest(lambda cand: count(keys >= cand), KSEL, (T, 1), LP)
        live = thr > INT_MIN
        excess = jnp.max(jnp.where(live, cnt_thr - KSEL, 0))
        bias_ref[0] = jnp.where((keys > thr) | ((keys == thr) & live), 0.0, NEG_BIG)

        @pl.when(excess > 0)
        def _():
            need = KSEL - count(keys > thr)
            cut = _tie_cut(lambda p: count((keys == thr) & (pos <= p)), need, NBITS, (T, 1))
            bias_ref[0] = jnp.where((keys > thr) | ((keys == thr) & live & (pos <= cut)), 0.0, NEG_BIG)


def _dsa_sample_attend_kernel(pt_ref, q_ref, bias_ref, biasn_ref, kn_ref, vn_ref, *rest, T, PAGE, G):
    k_refs = rest[:PAGES_PER_STEP]
    v_refs = rest[PAGES_PER_STEP:2 * PAGES_PER_STEP]
    o_ref, m_scr, l_scr, acc_scr = rest[2 * PAGES_PER_STEP:]
    j = pl.program_id(1)
    nj = pl.num_programs(1)
    R = N_KV * G * T
    q = q_ref[0]
    dn = (((1,), (1,)), ((), ()))

    @pl.when(j == 0)
    def _():
        m_scr[...] = jnp.full(m_scr.shape, NEG_BIG, F32)
        l_scr[...] = jnp.zeros(l_scr.shape, F32)
        acc_scr[...] = jnp.zeros(acc_scr.shape, F32)

    def accumulate(kblk, vblk, bias):
        s = lax.dot_general(q, kblk.astype(BF16), dn, preferred_element_type=F32)
        s = s + jnp.concatenate([bias] * (R // T), axis=0)
        m = m_scr[...]
        m_new = jnp.maximum(m, jnp.max(s, axis=1, keepdims=True))
        p = jnp.exp(s - m_new)
        corr = jnp.exp(m - m_new)
        l_scr[...] = l_scr[...] * corr + jnp.sum(p, axis=1, keepdims=True)
        acc_scr[...] = acc_scr[...] * corr + jnp.dot(p.astype(BF16), vblk.astype(BF16),
                                                      preferred_element_type=F32)
        m_scr[...] = m_new

    accumulate(jnp.concatenate([r[0, 0] for r in k_refs], axis=0),
               jnp.concatenate([r[0, 0] for r in v_refs], axis=0), bias_ref[0])

    @pl.when(j == nj - 1)
    def _():
        accumulate(kn_ref[0], vn_ref[0], biasn_ref[0])
        o = acc_scr[...] / l_scr[...]
        hd = o.shape[1] // N_KV
        lane_head = lax.broadcasted_iota(I32, (1, o.shape[1]), 1) // hd
        GT = G * T
        out = jnp.where(lane_head == 0, o[0:GT], 0.0)
        for n in range(1, N_KV):
            out = out + jnp.where(lane_head == n, o[n * GT:(n + 1) * GT], 0.0)
        o_ref[0] = out


def dsa_sample(q, k, v, qi, ki, wi, cache_k, cache_v, cache_kidx, page_table, layer):
    DB, T, H, HD = q.shape
    G = H // N_KV
    KW = N_KV * HD
    n_phys, PAGE = cache_k.shape[1], cache_k.shape[2]
    n_pages = page_table.shape[1]
    past = n_pages * PAGE
    assert n_pages % PAGES_PER_STEP == 0 and T <= PAGE and T % 8 == 0
    nJ = n_pages // PAGES_PER_STEP
    LP = past + PAGE
    KSEL = min(TOPK_MAX, (past + T) // 4)
    ck = cache_k.reshape(cache_k.shape[0], n_phys, PAGE, KW)
    cv = cache_v.reshape(cache_v.shape[0], n_phys, PAGE, KW)

    qh, ql = _hi_lo(qi.transpose(0, 2, 1, 3).reshape(DB, IDX_HEADS * T, IDX_DIM))
    wcol = wi.transpose(0, 2, 1).reshape(DB, IDX_HEADS * T, 1)
    kh, kl = _hi_lo(ki)
    kin = jnp.pad(jnp.concatenate([kh, kl, kh], axis=-1), ((0, 0), (0, PAGE - T), (0, 0)))

    def page_spec(width, i):
        return pl.BlockSpec((1, 1, PAGE, width),
                            lambda b, j, pt: (layer, pt[b, j * PAGES_PER_STEP + i], 0, 0))

    bias = pl.pallas_call(
        functools.partial(_dsa_sample_select_kernel, T=T, PAGE=PAGE, KSEL=KSEL,
                          NBITS=max(1, (LP - 1).bit_length())),
        grid_spec=pltpu.PrefetchScalarGridSpec(
            num_scalar_prefetch=1,
            grid=(DB, nJ),
            in_specs=[pl.BlockSpec((1, IDX_HEADS * T, IDX_DIM), lambda b, j, pt: (b, 0, 0)),
                      pl.BlockSpec((1, IDX_HEADS * T, IDX_DIM), lambda b, j, pt: (b, 0, 0)),
                      pl.BlockSpec((1, IDX_HEADS * T, 1), lambda b, j, pt: (b, 0, 0)),
                      pl.BlockSpec((1, PAGE, 3 * IDX_DIM), lambda b, j, pt: (b, 0, 0))]
                     + [page_spec(IDX_DIM, i) for i in range(PAGES_PER_STEP)],
            out_specs=pl.BlockSpec((1, T, LP), lambda b, j, pt: (b, 0, 0)),
            scratch_shapes=[pltpu.VMEM((T, LP), I32)]),
        out_shape=jax.ShapeDtypeStruct((DB, T, LP), F32),
        compiler_params=_cparams("parallel", "arbitrary"),
        name="dsa_sample_select",
    )(page_table, qh, ql, wcol, kin, *([cache_kidx] * PAGES_PER_STEP))

    qs = (q * (HD ** -0.5)).reshape(DB, T, N_KV, G, HD).transpose(0, 2, 3, 1, 4)
    qrows = jnp.einsum("bngtd,nm->bngtmd", qs, jnp.eye(N_KV, dtype=F32)).reshape(DB, N_KV * G * T, KW)
    qrows = qrows.astype(BF16)
    kn = jnp.pad(k.reshape(DB, T, KW), ((0, 0), (0, PAGE - T), (0, 0)))
    vn = jnp.pad(v.reshape(DB, T, KW), ((0, 0), (0, PAGE - T), (0, 0)))
    W = PAGES_PER_STEP * PAGE
    R = N_KV * G * T

    o = pl.pallas_call(
        functools.partial(_dsa_sample_attend_kernel, T=T, PAGE=PAGE, G=G),
        grid_spec=pltpu.PrefetchScalarGridSpec(
            num_scalar_prefetch=1,
            grid=(DB, nJ),
            in_specs=[pl.BlockSpec((1, R, KW), lambda b, j, pt: (b, 0, 0)),
                      pl.BlockSpec((1, T, W), lambda b, j, pt: (b, 0, j)),
                      pl.BlockSpec((1, T, PAGE), lambda b, j, pt: (b, 0, past // PAGE)),
                      pl.BlockSpec((1, PAGE, KW), lambda b, j, pt: (b, 0, 0)),
                      pl.BlockSpec((1, PAGE, KW), lambda b, j, pt: (b, 0, 0))]
                     + [page_spec(KW, i) for i in range(PAGES_PER_STEP)]
                     + [page_spec(KW, i) for i in range(PAGES_PER_STEP)],
            out_specs=pl.BlockSpec((1, G * T, KW), lambda b, j, pt: (b, 0, 0)),
            scratch_shapes=[pltpu.VMEM((R, 1), F32), pltpu.VMEM((R, 1), F32), pltpu.VMEM((R, KW), F32)]),
        out_shape=jax.ShapeDtypeStruct((DB, G * T, KW), F32),
        compiler_params=_cparams("parallel", "arbitrary"),
        name="dsa_sample_attend",
    )(page_table, qrows, bias, bias, kn, vn, *([ck] * PAGES_PER_STEP), *([cv] * PAGES_PER_STEP))
    o = o.reshape(DB, G, T, N_KV, HD).transpose(0, 2, 3, 1, 4)
    return o.reshape(DB, T, H * HD)


def _rope(x, pos):
    half = x.shape[-1] // 2
    inv = ROPE_THETA ** (-jnp.arange(half, dtype=F32) / half)
    ang = pos.astype(F32)[:, None] * inv[None, :]
    cos, sin = jnp.cos(ang)[:, None, :], jnp.sin(ang)[:, None, :]
    x1, x2 = x[..., :half], x[..., half:]
    return jnp.concatenate([x1 * cos - x2 * sin, x2 * cos + x1 * sin], axis=-1)


def _lower_bounds(logits):
    p = jax.nn.softmax(logits.astype(F32), axis=0)
    c = jnp.cumsum(p, axis=0)
    return c - c[0:1]


def _token_mixer(x, pos, wl, conv_buf, hg_s0, attend):
    B, L, D = x.shape
    cd = D // 4
    hw = D // 4
    aw = D // 2
    kvw = N_KV * HEAD_DIM
    n_heads = aw // HEAD_DIM
    z = matmul(x.reshape(B * L, D), wl["w_in"]).reshape(B, L, -1)
    o = 0
    def take(n):
        nonlocal o
        s = z[..., o:o + n]
        o += n
        return s
    c_h, c_b, c_c = take(cd), take(cd), take(cd)
    hg_col0 = o
    o += 4 * hw
    a_q, a_k, a_v = take(aw), take(kvw), take(kvw)
    a_qi, a_ki, a_w = take(IDX_HEADS * IDX_DIM), take(IDX_DIM), take(IDX_HEADS)
    u = c_c * c_h
    ext = jnp.concatenate([conv_buf.astype(u.dtype), u], axis=1)
    y = sum(wl["w_conv"][j] * ext[:, j:j + L] for j in range(CONV_W))
    y_conv = c_b * y
    conv_state = ext[:, ext.shape[1] - (CONV_W - 1):]
    y_hg, hg_state = hgrn2(z, hg_col0, wl["lower"], wl["hg_norm"], hg_s0)
    q = _rope(a_q.reshape(B, L, n_heads, HEAD_DIM), pos)
    k = _rope(a_k.reshape(B, L, N_KV, HEAD_DIM), pos)
    v = a_v.reshape(B, L, N_KV, HEAD_DIM)
    qi = _rope(a_qi.reshape(B, L, IDX_HEADS, IDX_DIM), pos) * (IDX_DIM ** -0.5)
    ki = _rope(a_ki[:, :, None, :], pos)[:, :, 0, :]
    wi = a_w * (IDX_HEADS ** -0.5)
    y_att = attend(q, k, v, qi, ki, wi)
    mix = jnp.concatenate([y_conv, y_hg, y_att], axis=-1)
    return mix, conv_state, hg_state, k, v, ki


def _layer_tail(h, mix, mk, mv, wl, alpha):
    B, T, D = h.shape
    h2 = matmul_res_ln(mix.reshape(B * T, -1), wl["w_out"], h.reshape(B * T, D), wl["ln1_g"], wl["ln1_b"],
                       alpha=alpha)
    xq = matmul(h2, wl["w_xq"]).reshape(B, T, -1)
    xo = cross_attention(xq, mk, mv)
    h3 = matmul_res_ln(xo.reshape(B * T, -1), wl["w_xo"], h2, wl["ln2_g"], wl["ln2_b"], alpha=alpha)
    u = matmul(h3, wl["w_up"], act="relu2", out_dtype=BF16)
    h4 = matmul_res_ln(u, wl["w_down"], h3, wl["ln3_g"], wl["ln3_b"], alpha=alpha)
    return h4.reshape(B, T, D)


def kernel(x_prompt, x_sample, cache_k, cache_v, cache_kidx, state_hgrn, state_conv, cache_mem_k, cache_mem_v,
           page_table, mem_prompt, w_in, w_conv, hg_lb_logits, hg_norm, w_out, ln1_g, ln1_b,
           w_xq, w_xk, w_xv, w_xo, ln2_g, ln2_b, w_up, w_down, ln3_g, ln3_b):
    depth = w_in.shape[0]
    Bp, Tp, D = x_prompt.shape
    DB, Ts, _ = x_sample.shape
    alpha = (2 * depth) ** 0.25
    past = page_table.shape[1] * cache_k.shape[2]
    pos_p = jnp.arange(Tp)
    pos_s = past + jnp.arange(Ts)
    lower = _lower_bounds(hg_lb_logits)
    n_in = w_in.shape[-1]
    n_in_pad = -(-n_in // (5 * LANES)) * (5 * LANES)
    n_mem = mem_prompt.shape[1]
    xw = w_xk.shape[-1]
    hp, hs = x_prompt, x_sample
    outs = [[] for _ in range(12)]
    for l in range(depth):
        wl = dict(
            w_in=jnp.pad(w_in[l], ((0, 0), (0, n_in_pad - n_in))).astype(BF16),
            w_conv=w_conv[l], lower=lower[l], hg_norm=hg_norm[l],
            w_out=w_out[l].astype(BF16), ln1_g=ln1_g[l], ln1_b=ln1_b[l],
            w_xq=w_xq[l].astype(BF16), w_xo=w_xo[l].astype(BF16), ln2_g=ln2_g[l], ln2_b=ln2_b[l],
            w_up=w_up[l].astype(BF16), w_down=w_down[l].astype(BF16), ln3_g=ln3_g[l], ln3_b=ln3_b[l])
        conv0 = jnp.zeros((Bp, CONV_W - 1, D // 4), F32)
        hg0 = jnp.zeros((Bp, HG_HEADS, D // 4 // HG_HEADS, D // 4 // HG_HEADS), F32)
        mix, c_st, h_st, k, v, ki = _token_mixer(hp, pos_p, wl, conv0, hg0, dsa_prompt)
        w_kv = jnp.concatenate([w_xk[l], w_xv[l]], axis=1).astype(BF16)
        mkv = matmul(mem_prompt.reshape(Bp * n_mem, D), w_kv).reshape(Bp, n_mem, 2 * xw)
        mk, mv = mkv[..., :xw], mkv[..., xw:]
        hp = _layer_tail(hp, mix, mk, mv, wl, alpha)
        for lst, a in zip(outs[:3] + [outs[6], outs[8]], (k, v, ki, h_st, c_st)):
            lst.append(a)
        outs[10].append(mk.reshape(Bp, n_mem, X_HEADS, xw // X_HEADS))
        outs[11].append(mv.reshape(Bp, n_mem, X_HEADS, xw // X_HEADS))
        attend = functools.partial(dsa_sample, cache_k=cache_k, cache_v=cache_v, cache_kidx=cache_kidx,
                                   page_table=page_table, layer=l)
        mix, c_st, h_st, k, v, ki = _token_mixer(hs, pos_s, wl, state_conv[l], state_hgrn[l], attend)
        hs = _layer_tail(hs, mix, cache_mem_k[l].reshape(DB, n_mem, xw), cache_mem_v[l].reshape(DB, n_mem, xw),
                         wl, alpha)
        for lst, a in zip(outs[3:6] + [outs[7], outs[9]], (k, v, ki, h_st, c_st)):
            lst.append(a)
    st = lambda xs: jnp.stack(xs, axis=0)
    kp, vp, kip, ksl, vsl, kis, hgp, hgs, cvp, cvs, mkp, mvp = (st(o) for o in outs)
    return (hp, hs, kp, vp, kip, ksl, vsl, kis, hgp, hgs, cvp, cvs, mkp, mvp)
```

```python
import functools

import numpy as np
import jax
import jax.numpy as jnp
from jax import lax
from jax.experimental import pallas as pl
from jax.experimental.pallas import tpu as pltpu

F32 = jnp.float32
BF16 = jnp.bfloat16
I32 = jnp.int32
I16 = jnp.int16

HG_HEADS = 4
HEAD_DIM = 64
N_KV = 4
IDX_HEADS = 8
IDX_DIM = 32
TOPK_MAX = 256
X_HEADS = 4
CONV_W = 3
ROPE_THETA = 10000.0
LN_EPS = 1e-5

LANES = 128
PACK = 16
VMEM_LIMIT = 56 * 1024 * 1024
INT_MIN = -2 ** 31
LOG2E = 1.4426950408889634
NEG_BIG = -1e30


def _cparams(*sem):
    return pltpu.CompilerParams(dimension_semantics=sem, vmem_limit_bytes=VMEM_LIMIT)


def _mm_kernel(x_ref, w_ref, o_ref, *, act):
    acc = jnp.dot(x_ref[...].astype(BF16), w_ref[...], preferred_element_type=F32)
    if act == "relu2":
        r = jnp.maximum(acc, 0.0)
        acc = r * r
    o_ref[...] = acc.astype(o_ref.dtype)


def matmul(x, w, *, act=None, out_dtype=F32, tm=512, tn=None):
    M, K = x.shape
    N = w.shape[1]
    tm = min(tm, M)
    if tn is None:
        tn = next(c for c in range(min(N, 1024), 0, -LANES) if N % c == 0)
    assert M % tm == 0 and N % tn == 0
    return pl.pallas_call(
        functools.partial(_mm_kernel, act=act),
        grid=(M // tm, N // tn),
        in_specs=[pl.BlockSpec((tm, K), lambda i, j: (i, 0)),
                  pl.BlockSpec((K, tn), lambda i, j: (0, j))],
        out_specs=pl.BlockSpec((tm, tn), lambda i, j: (i, j)),
        out_shape=jax.ShapeDtypeStruct((M, N), out_dtype),
        compiler_params=_cparams("parallel", "arbitrary"),
        name="matmul",
    )(x, w)


def _mm_ln_kernel(x_ref, w_ref, r_ref, g_ref, b_ref, o_ref, *, alpha):
    acc = jnp.dot(x_ref[...].astype(BF16), w_ref[...], preferred_element_type=F32)
    h = alpha * r_ref[...] + acc
    mu = jnp.mean(h, axis=-1, keepdims=True)
    d = h - mu
    var = jnp.mean(d * d, axis=-1, keepdims=True)
    o_ref[...] = d * lax.rsqrt(var + LN_EPS) * g_ref[...] + b_ref[...]


def matmul_res_ln(x, w, res, g, b, *, alpha, tm=256):
    M, K = x.shape
    N = w.shape[1]
    tm = min(tm, M)
    assert M % tm == 0
    return pl.pallas_call(
        functools.partial(_mm_ln_kernel, alpha=alpha),
        grid=(M // tm,),
        in_specs=[pl.BlockSpec((tm, K), lambda i: (i, 0)),
                  pl.BlockSpec((K, N), lambda i: (0, 0)),
                  pl.BlockSpec((tm, N), lambda i: (i, 0)),
                  pl.BlockSpec((1, N), lambda i: (0, 0)),
                  pl.BlockSpec((1, N), lambda i: (0, 0))],
        out_specs=pl.BlockSpec((tm, N), lambda i: (i, 0)),
        out_shape=jax.ShapeDtypeStruct((M, N), F32),
        compiler_params=_cparams("parallel"),
        name="matmul_res_ln",
    )(x, w, res, g.reshape(1, N), b.reshape(1, N))


def _xattn_kernel(q_ref, mk_ref, mv_ref, o_ref, *, hd):
    scale = hd ** -0.5
    for h in range(X_HEADS):
        sl = slice(h * hd, (h + 1) * hd)
        q = q_ref[0, :, sl].astype(BF16)
        k = mk_ref[0, :, sl].astype(BF16)
        v = mv_ref[0, :, sl].astype(BF16)
        s = lax.dot_general(q, k, (((1,), (1,)), ((), ())), preferred_element_type=F32) * scale
        m = jnp.max(s, axis=-1, keepdims=True)
        p = jnp.exp(s - m)
        l = jnp.sum(p, axis=-1, keepdims=True)
        o = jnp.dot(p.astype(BF16), v, preferred_element_type=F32)
        o_ref[0, :, sl] = o / l


def cross_attention(q, mk, mv, *, tq=512):
    B, T, XW = q.shape
    NM = mk.shape[1]
    tq = min(tq, T)
    assert T % tq == 0
    return pl.pallas_call(
        functools.partial(_xattn_kernel, hd=XW // X_HEADS),
        grid=(B, T // tq),
        in_specs=[pl.BlockSpec((1, tq, XW), lambda b, i: (b, i, 0)),
                  pl.BlockSpec((1, NM, XW), lambda b, i: (b, 0, 0)),
                  pl.BlockSpec((1, NM, XW), lambda b, i: (b, 0, 0))],
        out_specs=pl.BlockSpec((1, tq, XW), lambda b, i: (b, i, 0)),
        out_shape=jax.ShapeDtypeStruct((B, T, XW), F32),
        compiler_params=_cparams("parallel", "arbitrary"),
        name="cross_attention",
    )(q, mk, mv)


def _split3(x):
    a = x.astype(BF16)
    r = x - a.astype(F32)
    b = r.astype(BF16)
    c = (r - b.astype(F32)).astype(BF16)
    return a, b, c


def _hgrn_kernel(gq_ref, gf_ref, gi_ref, gg_ref, lb_ref, nrm_ref, s0_ref, y_ref, st_ref, s_scr,
                 *, TB, CH, SB, HW):
    dk = HW // HG_HEADS
    c = pl.program_id(1)

    @pl.when(c == 0)
    def _():
        s_scr[...] = s0_ref[0]

    lb = lb_ref[...]
    lane_head = lax.broadcasted_iota(I32, (1, HW), 1) // dk
    tril = (lax.broadcasted_iota(I32, (CH, CH), 0) >= lax.broadcasted_iota(I32, (CH, CH), 1)
            ).astype(BF16)
    blk = (lax.broadcasted_iota(I32, (HW, HW), 0) // dk == lax.broadcasted_iota(I32, (HW, HW), 1) // dk)
    gmean = jnp.where(blk, 1.0 / dk, 0.0).astype(BF16)
    row_ch = lax.broadcasted_iota(I32, (CH, 1), 0)
    NB = CH // SB
    S = s_scr[...]
    for cc in range(TB // CH):
        sl = slice(cc * CH, (cc + 1) * CH)
        gq = gq_ref[0, sl, :]
        gi = gi_ref[0, sl, :]
        sig = jax.nn.sigmoid(gf_ref[0, sl, :])
        f = lb + (1.0 - lb) * sig
        kk = (1.0 - lb) * (1.0 - sig)
        lf = jnp.log(f)
        l1, l2, l3 = _split3(lf)
        b = (jnp.dot(tril, l1, preferred_element_type=F32)
             + jnp.dot(tril, l2, preferred_element_type=F32)
             + jnp.dot(tril, l3, preferred_element_type=F32))
        gi_b = gi.astype(BF16)
        qb = (gq * jnp.exp(b)).astype(BF16)
        o_parts = lax.dot_general(qb, S.astype(BF16), (((1,), (1,)), ((), ())),
                                  preferred_element_type=F32)
        outs = []
        for i in range(NB):
            rows = slice(i * SB, (i + 1) * SB)
            nk = (i + 1) * SB
            m_i = b[i * SB + SB // 2 - 1: i * SB + SB // 2, :]
            qh = gq[rows] * jnp.exp(b[rows] - m_i)
            qi = jnp.concatenate([jnp.where(lane_head == h, qh, 0.0) for h in range(HG_HEADS)],
                                 axis=0).astype(BF16)
            ki = (kk[:nk] * jnp.exp(m_i - b[:nk])).astype(BF16)
            att = lax.dot_general(qi, ki, (((1,), (1,)), ((), ())),
                                  preferred_element_type=F32)
            tq = lax.broadcasted_iota(I32, (HG_HEADS * SB, nk), 0) % SB + i * SB
            ts = lax.broadcasted_iota(I32, (HG_HEADS * SB, nk), 1)
            att = jnp.where(ts <= tq, att, 0.0).astype(BF16)
            oi = jnp.dot(att, gi_b[:nk], preferred_element_type=F32)
            acc = jnp.where(lane_head == 0, oi[0:SB], 0.0)
            for h in range(1, HG_HEADS):
                acc = acc + jnp.where(lane_head == h, oi[h * SB:(h + 1) * SB], 0.0)
            outs.append(acc)
        o = o_parts + (jnp.concatenate(outs, axis=0) if NB > 1 else outs[0])
        bC = b[CH - 1:CH, :]
        kdec = (kk * jnp.exp(bC - b)).astype(BF16)
        upd = lax.dot_general(gi_b, kdec, (((0,), (0,)), ((), ())),
                              preferred_element_type=F32)
        S = jnp.where(blk, S * jnp.exp(bC) + upd, 0.0)
        o2 = o * o
        o2a = o2.astype(BF16)
        o2b = (o2 - o2a.astype(F32)).astype(BF16)
        ms = (jnp.dot(o2a, gmean, preferred_element_type=F32)
              + jnp.dot(o2b, gmean, preferred_element_type=F32))
        y = o * lax.rsqrt(ms + LN_EPS) * nrm_ref[...] * jax.nn.sigmoid(gg_ref[0, sl, :])
        y_ref[0, sl, :] = y
    s_scr[...] = S

    @pl.when(c == pl.num_programs(1) - 1)
    def _():
        st_ref[0] = S


def hgrn2(z, col0, lb, nrm, s0):
    B, L, _ = z.shape
    HW = lb.shape[-1]
    dk = HW // HG_HEADS
    assert col0 % HW == 0
    cb = col0 // HW
    if L % 64 == 0:
        CH, SB = 64, 16
        TB = 256 if L % 256 == 0 else 64
    else:
        assert L % 8 == 0 and L <= 64
        CH, SB, TB = L, 8, L
    eye = jnp.eye(HG_HEADS, dtype=F32)
    s0t = jnp.einsum("bhkv,hg->bgvhk", s0.astype(F32), eye).reshape(B, HW, HW)
    zspec = lambda k: pl.BlockSpec((1, TB, HW), lambda b, c: (b, c, cb + k))
    y, st = pl.pallas_call(
        functools.partial(_hgrn_kernel, TB=TB, CH=CH, SB=SB, HW=HW),
        grid=(B, L // TB),
        in_specs=[zspec(0), zspec(1), zspec(2), zspec(3),
                  pl.BlockSpec((1, HW), lambda b, c: (0, 0)),
                  pl.BlockSpec((1, HW), lambda b, c: (0, 0)),
                  pl.BlockSpec((1, HW, HW), lambda b, c: (b, 0, 0))],
        out_specs=[pl.BlockSpec((1, TB, HW), lambda b, c: (b, c, 0)),
                   pl.BlockSpec((1, HW, HW), lambda b, c: (b, 0, 0))],
        out_shape=[jax.ShapeDtypeStruct((B, L, HW), F32),
                   jax.ShapeDtypeStruct((B, HW, HW), F32)],
        scratch_shapes=[pltpu.VMEM((HW, HW), F32)],
        compiler_params=_cparams("parallel", "arbitrary"),
        name="hgrn2",
    )(z, z, z, z, lb.reshape(1, HW), nrm.reshape(1, HW), s0t)
    st = st.reshape(B, HG_HEADS, dk, HG_HEADS, dk)
    state = jnp.einsum("bgvhk,hg->bhkv", st, eye)
    return y, state


def _sort_key(score):
    bits = pltpu.bitcast(score, I32)
    return jnp.where(bits < 0, bits ^ jnp.int32(0x7FFFFFFF), bits)


def _kth_largest(count_ge, ksel, shape, total, nbits=32):
    def body(i, carry):
        thr, cnt_thr = carry
        cand = thr + jnp.left_shift(jnp.int32(1), nbits - 1 - i)
        cnt = count_ge(cand)
        ok = cnt >= ksel
        return jnp.where(ok, cand, thr), jnp.where(ok, cnt, cnt_thr)
    init = (jnp.full(shape, -2 ** (nbits - 1), I32), jnp.zeros(shape, I32) + total)
    return lax.fori_loop(0, nbits, body, init)


def _tie_cut(count_eq_le, need, nbits, shape):
    def body(i, p):
        cand = p + jnp.left_shift(jnp.int32(1), nbits - 1 - i)
        return jnp.where(count_eq_le(cand - 1) < need, cand, p)
    return lax.fori_loop(0, nbits, body, jnp.zeros(shape, I32))


def _dsa_prompt_kernel(kia_ref, qit_ref, w_ref, k_ref, qt_ref, vt_ref, o_ref, keys_ref, hi_ref, lo_ref, bias_ref,
                       s_ref, *, QB, TS, KSEL, NBITS):
    qb = pl.program_id(1)
    n_t = ((qb + 1) * QB + TS - 1) // TS
    t_idx = qb * QB + lax.broadcasted_iota(I32, (1, QB), 1)
    qit = qit_ref[0, 0]
    wrow = w_ref[0, 0]

    def row0(j):
        return pl.multiple_of(j * TS, TS)

    def s_index(j):
        return j * TS + lax.broadcasted_iota(I32, (TS, QB), 0)

    def score_tile(j, _):
        ka = kia_ref[0, pl.ds(row0(j), TS), :]
        d = jnp.dot(ka, qit, preferred_element_type=F32)
        r = jnp.maximum(d, 0.0) * wrow
        sc = r[:, 0:QB]
        for h in range(1, IDX_HEADS):
            sc = sc + r[:, h * QB:(h + 1) * QB]
        key = jnp.where(s_index(j) <= t_idx, _sort_key(sc), INT_MIN)
        keys_ref[pl.ds(row0(j), TS), :] = key
        hi_ref[pl.ds(row0(j), TS), :] = (key >> 16).astype(I16)
        lo_ref[pl.ds(row0(j), TS), :] = ((key & 0xFFFF) - 32768).astype(I16)
        return 0

    lax.fori_loop(0, n_t, score_tile, 0)

    def count(pred):
        def body(j, acc):
            m = pred(keys_ref[pl.ds(row0(j), TS), :], j).astype(I32)
            return acc + jnp.sum(m.reshape(TS // 8, 8, QB), axis=0)
        acc = lax.fori_loop(0, n_t, body, jnp.zeros((8, QB), I32))
        return jnp.sum(acc, axis=0, keepdims=True)

    def count16(ref, pred):
        def body(j, acc):
            m = jnp.where(pred(ref[pl.ds(row0(j), TS), :]), jnp.int16(1), jnp.int16(0))
            parts = [m[u * PACK:(u + 1) * PACK] for u in range(TS // PACK)]
            while len(parts) > 1:
                parts = [parts[a] + parts[a + 1] for a in range(0, len(parts), 2)]
            return acc + parts[0]
        acc = lax.fori_loop(0, n_t, body, jnp.zeros((PACK, QB), I16))
        return jnp.sum(acc.astype(I32), axis=0, keepdims=True)

    thr_hi, cnt_ge_hi = _kth_largest(lambda c: count16(hi_ref, lambda t: t >= c.astype(I16)),
                                     KSEL, (1, QB), n_t * TS, nbits=16)
    thr_hi16 = thr_hi.astype(I16)
    cnt_gt_hi = count16(hi_ref, lambda t: t > thr_hi16)

    def bucket_tile(j, _):
        sl = pl.ds(row0(j), TS)
        lo_ref[sl, :] = jnp.where(hi_ref[sl, :] == thr_hi16, lo_ref[sl, :], jnp.int16(-32768))
        return 0

    lax.fori_loop(0, n_t, bucket_tile, 0)
    thr_lo, cnt_lo = _kth_largest(lambda c: count16(lo_ref, lambda t: t >= c.astype(I16)),
                                  KSEL - cnt_gt_hi, (1, QB), cnt_ge_hi - cnt_gt_hi, nbits=16)
    thr = jnp.left_shift(thr_hi, 16) | (thr_lo + 32768)
    cnt_thr = cnt_gt_hi + cnt_lo
    live = thr > INT_MIN
    excess = jnp.max(jnp.where(live, cnt_thr - KSEL, 0))

    @pl.when(excess <= 0)
    def _():
        thr_all = jnp.where(live, thr, INT_MIN + 1)

        def bias_tile(j, _):
            key = keys_ref[pl.ds(row0(j), TS), :]
            bias_ref[pl.ds(row0(j), TS), :] = jnp.where(key >= thr_all, 0.0, NEG_BIG)
            return 0

        lax.fori_loop(0, n_t, bias_tile, 0)

    @pl.when(excess > 0)
    def _():
        need = KSEL - count(lambda key, j: key > thr)
        cut = _tie_cut(lambda p: count(lambda key, j: (key == thr) & (s_index(j) <= p)), need, NBITS, (1, QB))
        cut = jnp.where(live, cut, -1)

        def bias_tile(j, _):
            key = keys_ref[pl.ds(row0(j), TS), :]
            sel = (key > thr) | ((key == thr) & (s_index(j) <= cut))
            bias_ref[pl.ds(row0(j), TS), :] = jnp.where(sel, 0.0, NEG_BIG)
            return 0

        lax.fori_loop(0, n_t, bias_tile, 0)

    GQ = qt_ref.shape[-1]
    hd = qt_ref.shape[-2]
    G = GQ // QB

    def logit_tile(j, m8s):
        bias = bias_ref[pl.ds(row0(j), TS), :]
        bias = jnp.concatenate([bias] * G, axis=1)
        out = []
        for n in range(N_KV):
            s = jnp.dot(k_ref[0, n, pl.ds(row0(j), TS), :], qt_ref[0, 0, n], preferred_element_type=F32) + bias
            s_ref[n, pl.ds(row0(j), TS), :] = s
            out.append(jnp.maximum(m8s[n], jnp.max(s.reshape(TS // 8, 8, GQ), axis=0)))
        return tuple(out)

    m8s = lax.fori_loop(0, n_t, logit_tile, tuple(jnp.full((8, GQ), NEG_BIG, F32) for _ in range(N_KV)))
    ms = [jnp.max(m8, axis=0, keepdims=True) for m8 in m8s]

    def att_tile(j, carry):
        out = []
        for n in range(N_KV):
            l8, acc = carry[n]
            p = jnp.exp2(s_ref[n, pl.ds(row0(j), TS), :] - ms[n])
            l8 = l8 + jnp.sum(p.reshape(TS // 8, 8, GQ), axis=0)
            vt = vt_ref[0, n, :, pl.ds(row0(j), TS)]
            acc = acc + jnp.dot(vt, p.astype(BF16), preferred_element_type=F32)
            out.append((l8, acc))
        return tuple(out)

    res = lax.fori_loop(0, n_t, att_tile,
                        tuple((jnp.zeros((8, GQ), F32), jnp.zeros((hd, GQ), F32)) for _ in range(N_KV)))
    for n in range(N_KV):
        l8, acc = res[n]
        o_ref[0, 0, n] = acc / jnp.sum(l8, axis=0, keepdims=True)


def _hi_lo(x):
    hi = x.astype(BF16)
    lo = (x - hi.astype(F32)).astype(BF16)
    return hi, lo


def dsa_prompt(q, k, v, qi, ki, wi):
    B, T, H, HD = q.shape
    G = H // N_KV
    QB = 128
    TS = min(512, T)
    assert T % QB == 0 and T % TS == 0
    nQ = T // QB
    KSEL = min(TOPK_MAX, T // 4)
    assert TS >= KSEL
    DI3 = 3 * IDX_DIM
    qh, ql = _hi_lo(qi)
    kh, kl = _hi_lo(ki)
    qi3 = jnp.concatenate([qh, qh, ql], axis=-1)
    kia = jnp.concatenate([kh, kl, kh], axis=-1)
    qit = qi3.reshape(B, nQ, QB, IDX_HEADS, DI3).transpose(0, 1, 4, 3, 2).reshape(B, nQ, DI3, IDX_HEADS * QB)
    wrow = wi.reshape(B, nQ, QB, IDX_HEADS).transpose(0, 1, 3, 2).reshape(B, nQ, 1, IDX_HEADS * QB)
    qs = (q * (HD ** -0.5 * LOG2E)).astype(BF16)
    qt = qs.reshape(B, nQ, QB, N_KV, G, HD).transpose(0, 1, 3, 5, 4, 2).reshape(B, nQ, N_KV, HD, G * QB)
    kb = k.astype(BF16).transpose(0, 2, 1, 3)
    vt = v.astype(BF16).transpose(0, 2, 3, 1)
    o = pl.pallas_call(
        functools.partial(_dsa_prompt_kernel, QB=QB, TS=TS, KSEL=KSEL,
                          NBITS=max(1, (T - 1).bit_length())),
        grid=(B, nQ),
        in_specs=[pl.BlockSpec((1, T, DI3), lambda b, i: (b, 0, 0)),
                  pl.BlockSpec((1, 1, DI3, IDX_HEADS * QB), lambda b, i: (b, i, 0, 0)),
                  pl.BlockSpec((1, 1, 1, IDX_HEADS * QB), lambda b, i: (b, i, 0, 0)),
                  pl.BlockSpec((1, N_KV, T, HD), lambda b, i: (b, 0, 0, 0)),
                  pl.BlockSpec((1, 1, N_KV, HD, G * QB), lambda b, i: (b, i, 0, 0, 0)),
                  pl.BlockSpec((1, N_KV, HD, T), lambda b, i: (b, 0, 0, 0))],
        out_specs=pl.BlockSpec((1, 1, N_KV, HD, G * QB), lambda b, i: (b, i, 0, 0, 0)),
        out_shape=jax.ShapeDtypeStruct((B, nQ, N_KV, HD, G * QB), F32),
        scratch_shapes=[pltpu.VMEM((T, QB), I32), pltpu.VMEM((T, QB), I16), pltpu.VMEM((T, QB), I16),
                        pltpu.VMEM((T, QB), F32), pltpu.VMEM((N_KV, T, G * QB), F32)],
        compiler_params=_cparams("parallel", "arbitrary"),
        name="dsa_prompt",
    )(kia, qit, wrow, kb, qt, vt)
    o = o.reshape(B, nQ, N_KV, HD, G, QB).transpose(0, 1, 5, 2, 4, 3)
    return o.reshape(B, T, H * HD)


PAGES_PER_STEP = 8


def _dsa_sample_select_kernel(pt_ref, qh_ref, ql_ref, w_ref, kin_ref, *rest, T, PAGE, KSEL, NBITS):
    kx_refs = rest[:PAGES_PER_STEP]
    bias_ref, keys_ref = rest[PAGES_PER_STEP:]
    j = pl.program_id(1)
    nj = pl.num_programs(1)
    W = PAGES_PER_STEP * PAGE
    LP = keys_ref.shape[1]
    past = LP - PAGE
    qh = qh_ref[0]
    ql = ql_ref[0]
    w = w_ref[0]
    dn = (((1,), (1,)), ((), ()))

    def scores(d):
        r = jnp.maximum(d, 0.0) * w
        sc = r[0:T]
        for h in range(1, IDX_HEADS):
            sc = sc + r[h * T:(h + 1) * T]
        return sc

    kx = jnp.concatenate([r[0, 0] for r in kx_refs], axis=0)
    kh, kl = _hi_lo(kx)
    d = (lax.dot_general(qh, kh, dn, preferred_element_type=F32)
         + lax.dot_general(qh, kl, dn, preferred_element_type=F32)
         + lax.dot_general(ql, kh, dn, preferred_element_type=F32))
    keys_ref[:, pl.ds(pl.multiple_of(j * W, W), W)] = _sort_key(scores(d))

    @pl.when(j == nj - 1)
    def _():
        kn = kin_ref[0]
        q3 = jnp.concatenate([qh, qh, ql], axis=1)
        dnew = lax.dot_general(q3, kn, dn, preferred_element_type=F32)
        jn = lax.broadcasted_iota(I32, (T, PAGE), 1)
        tq = lax.broadcasted_iota(I32, (T, PAGE), 0)
        keys_ref[:, past:] = jnp.where(jn <= tq, _sort_key(scores(dnew)), INT_MIN)
        keys = keys_ref[...]
        pos = lax.broadcasted_iota(I32, (T, LP), 1)

        def count(m):
            return jnp.sum(m.astype(I32), axis=1, keepdims=True)

        thr, cnt_thr = _kth_largest(lambda cand: count(keys >= cand), KSEL, (T, 1), LP)
        live = thr > INT_MIN
        excess = jnp.max(jnp.where(live, cnt_thr - KSEL, 0))
        bias_ref[0] = jnp.where((keys > thr) | ((keys == thr) & live), 0.0, NEG_BIG)

        @pl.when(excess > 0)
        def _():
            need = KSEL - count(keys > thr)
            cut = _tie_cut(lambda p: count((keys == thr) & (pos <= p)), need, NBITS, (T, 1))
            bias_ref[0] = jnp.where((keys > thr) | ((keys == thr) & live & (pos <= cut)), 0.0, NEG_BIG)


def _dsa_sample_attend_kernel(pt_ref, q_ref, bias_ref, biasn_ref, kn_ref, vn_ref, *rest, T, PAGE, G):
    k_refs = rest[:PAGES_PER_STEP]
    v_refs = rest[PAGES_PER_STEP:2 * PAGES_PER_STEP]
    o_ref, m_scr, l_scr, acc_scr = rest[2 * PAGES_PER_STEP:]
    j = pl.program_id(1)
    nj = pl.num_programs(1)
    R = N_KV * G * T
    q = q_ref[0]
    dn = (((1,), (1,)), ((), ()))

    @pl.when(j == 0)
    def _():
        m_scr[...] = jnp.full(m_scr.shape, NEG_BIG, F32)
        l_scr[...] = jnp.zeros(l_scr.shape, F32)
        acc_scr[...] = jnp.zeros(acc_scr.shape, F32)

    def accumulate(kblk, vblk, bias):
        s = lax.dot_general(q, kblk.astype(BF16), dn, preferred_element_type=F32)
        s = s + jnp.concatenate([bias] * (R // T), axis=0)
        m = m_scr[...]
        m_new = jnp.maximum(m, jnp.max(s, axis=1, keepdims=True))
        p = jnp.exp(s - m_new)
        corr = jnp.exp(m - m_new)
        l_scr[...] = l_scr[...] * corr + jnp.sum(p, axis=1, keepdims=True)
        acc_scr[...] = acc_scr[...] * corr + jnp.dot(p.astype(BF16), vblk.astype(BF16),
                                                      preferred_element_type=F32)
        m_scr[...] = m_new

    accumulate(jnp.concatenate([r[0, 0] for r in k_refs], axis=0),
               jnp.concatenate([r[0, 0] for r in v_refs], axis=0), bias_ref[0])

    @pl.when(j == nj - 1)
    def _():
        accumulate(kn_ref[0], vn_ref[0], biasn_ref[0])
        o = acc_scr[...] / l_scr[...]
        hd = o.shape[1] // N_KV
        lane_head = lax.broadcasted_iota(I32, (1, o.shape[1]), 1) // hd
        GT = G * T
        out = jnp.where(lane_head == 0, o[0:GT], 0.0)
        for n in range(1, N_KV):
            out = out + jnp.where(lane_head == n, o[n * GT:(n + 1) * GT], 0.0)
        o_ref[0] = out


def dsa_sample(q, k, v, qi, ki, wi, cache_k, cache_v, cache_kidx, page_table, layer):
    DB, T, H, HD = q.shape
    G = H // N_KV
    KW = N_KV * HD
    n_phys, PAGE = cache_k.shape[1], cache_k.shape[2]
    n_pages = page_table.shape[1]
    past = n_pages * PAGE
    assert n_pages % PAGES_PER_STEP == 0 and T <= PAGE and T % 8 == 0
    nJ = n_pages // PAGES_PER_STEP
    LP = past + PAGE
    KSEL = min(TOPK_MAX, (past + T) // 4)
    ck = cache_k.reshape(cache_k.shape[0], n_phys, PAGE, KW)
    cv = cache_v.reshape(cache_v.shape[0], n_phys, PAGE, KW)

    qh, ql = _hi_lo(qi.transpose(0, 2, 1, 3).reshape(DB, IDX_HEADS * T, IDX_DIM))
    wcol = wi.transpose(0, 2, 1).reshape(DB, IDX_HEADS * T, 1)
    kh, kl = _hi_lo(ki)
    kin = jnp.pad(jnp.concatenate([kh, kl, kh], axis=-1), ((0, 0), (0, PAGE - T), (0, 0)))

    def page_spec(width, i):
        return pl.BlockSpec((1, 1, PAGE, width),
                            lambda b, j, pt: (layer, pt[b, j * PAGES_PER_STEP + i], 0, 0))

    bias = pl.pallas_call(
        functools.partial(_dsa_sample_select_kernel, T=T, PAGE=PAGE, KSEL=KSEL,
                          NBITS=max(1, (LP - 1).bit_length())),
        grid_spec=pltpu.PrefetchScalarGridSpec(
            num_scalar_prefetch=1,
            grid=(DB, nJ),
            in_specs=[pl.BlockSpec((1, IDX_HEADS * T, IDX_DIM), lambda b, j, pt: (b, 0, 0)),
                      pl.BlockSpec((1, IDX_HEADS * T, IDX_DIM), lambda b, j, pt: (b, 0, 0)),
                      pl.BlockSpec((1, IDX_HEADS * T, 1), lambda b, j, pt: (b, 0, 0)),
                      pl.BlockSpec((1, PAGE, 3 * IDX_DIM), lambda b, j, pt: (b, 0, 0))]
                     + [page_spec(IDX_DIM, i) for i in range(PAGES_PER_STEP)],
            out_specs=pl.BlockSpec((1, T, LP), lambda b, j, pt: (b, 0, 0)),
            scratch_shapes=[pltpu.VMEM((T, LP), I32)]),
        out_shape=jax.ShapeDtypeStruct((DB, T, LP), F32),
        compiler_params=_cparams("parallel", "arbitrary"),
        name="dsa_sample_select",
    )(page_table, qh, ql, wcol, kin, *([cache_kidx] * PAGES_PER_STEP))

    qs = (q * (HD ** -0.5)).reshape(DB, T, N_KV, G, HD).transpose(0, 2, 3, 1, 4)
    qrows = jnp.einsum("bngtd,nm->bngtmd", qs, jnp.eye(N_KV, dtype=F32)).reshape(DB, N_KV * G * T, KW)
    qrows = qrows.astype(BF16)
    kn = jnp.pad(k.reshape(DB, T, KW), ((0, 0), (0, PAGE - T), (0, 0)))
    vn = jnp.pad(v.reshape(DB, T, KW), ((0, 0), (0, PAGE - T), (0, 0)))
    W = PAGES_PER_STEP * PAGE
    R = N_KV * G * T

    o = pl.pallas_call(
        functools.partial(_dsa_sample_attend_kernel, T=T, PAGE=PAGE, G=G),
        grid_spec=pltpu.PrefetchScalarGridSpec(
            num_scalar_prefetch=1,
            grid=(DB, nJ),
            in_specs=[pl.BlockSpec((1, R, KW), lambda b, j, pt: (b, 0, 0)),
                      pl.BlockSpec((1, T, W), lambda b, j, pt: (b, 0, j)),
                      pl.BlockSpec((1, T, PAGE), lambda b, j, pt: (b, 0, past // PAGE)),
                      pl.BlockSpec((1, PAGE, KW), lambda b, j, pt: (b, 0, 0)),
                      pl.BlockSpec((1, PAGE, KW), lambda b, j, pt: (b, 0, 0))]
                     + [page_spec(KW, i) for i in range(PAGES_PER_STEP)]
                     + [page_spec(KW, i) for i in range(PAGES_PER_STEP)],
            out_specs=pl.BlockSpec((1, G * T, KW), lambda b, j, pt: (b, 0, 0)),
            scratch_shapes=[pltpu.VMEM((R, 1), F32), pltpu.VMEM((R, 1), F32), pltpu.VMEM((R, KW), F32)]),
        out_shape=jax.ShapeDtypeStruct((DB, G * T, KW), F32),
        compiler_params=_cparams("parallel", "arbitrary"),
        name="dsa_sample_attend",
    )(page_table, qrows, bias, bias, kn, vn, *([ck] * PAGES_PER_STEP), *([cv] * PAGES_PER_STEP))
    o = o.reshape(DB, G, T, N_KV, HD).transpose(0, 2, 3, 1, 4)
    return o.reshape(DB, T, H * HD)


def _rope(x, pos):
    half = x.shape[-1] // 2
    inv = ROPE_THETA ** (-jnp.arange(half, dtype=F32) / half)
    ang = pos.astype(F32)[:, None] * inv[None, :]
    cos, sin = jnp.cos(ang)[:, None, :], jnp.sin(ang)[:, None, :]
    x1, x2 = x[..., :half], x[..., half:]
    return jnp.concatenate([x1 * cos - x2 * sin, x2 * cos + x1 * sin], axis=-1)


def _lower_bounds(logits):
    p = jax.nn.softmax(logits.astype(F32), axis=0)
    c = jnp.cumsum(p, axis=0)
    return c - c[0:1]


def _token_mixer(x, pos, wl, conv_buf, hg_s0, attend):
    B, L, D = x.shape
    cd = D // 4
    hw = D // 4
    aw = D // 2
    kvw = N_KV * HEAD_DIM
    n_heads = aw // HEAD_DIM
    z = matmul(x.reshape(B * L, D), wl["w_in"]).reshape(B, L, -1)
    o = 0
    def take(n):
        nonlocal o
        s = z[..., o:o + n]
        o += n
        return s
    c_h, c_b, c_c = take(cd), take(cd), take(cd)
    hg_col0 = o
    o += 4 * hw
    a_q, a_k, a_v = take(aw), take(kvw), take(kvw)
    a_qi, a_ki, a_w = take(IDX_HEADS * IDX_DIM), take(IDX_DIM), take(IDX_HEADS)
    u = c_c * c_h
    ext = jnp.concatenate([conv_buf.astype(u.dtype), u], axis=1)
    y = sum(wl["w_conv"][j] * ext[:, j:j + L] for j in range(CONV_W))
    y_conv = c_b * y
    conv_state = ext[:, ext.shape[1] - (CONV_W - 1):]
    y_hg, hg_state = hgrn2(z, hg_col0, wl["lower"], wl["hg_norm"], hg_s0)
    q = _rope(a_q.reshape(B, L, n_heads, HEAD_DIM), pos)
    k = _rope(a_k.reshape(B, L, N_KV, HEAD_DIM), pos)
    v = a_v.reshape(B, L, N_KV, HEAD_DIM)
    qi = _rope(a_qi.reshape(B, L, IDX_HEADS, IDX_DIM), pos) * (IDX_DIM ** -0.5)
    ki = _rope(a_ki[:, :, None, :], pos)[:, :, 0, :]
    wi = a_w * (IDX_HEADS ** -0.5)
    y_att = attend(q, k, v, qi, ki, wi)
    mix = jnp.concatenate([y_conv, y_hg, y_att], axis=-1)
    return mix, conv_state, hg_state, k, v, ki


def _layer_tail(h, mix, mk, mv, wl, alpha):
    B, T, D = h.shape
    h2 = matmul_res_ln(mix.reshape(B * T, -1), wl["w_out"], h.reshape(B * T, D), wl["ln1_g"], wl["ln1_b"],
                       alpha=alpha)
    xq = matmul(h2, wl["w_xq"]).reshape(B, T, -1)
    xo = cross_attention(xq, mk, mv)
    h3 = matmul_res_ln(xo.reshape(B * T, -1), wl["w_xo"], h2, wl["ln2_g"], wl["ln2_b"], alpha=alpha)
    u = matmul(h3, wl["w_up"], act="relu2", out_dtype=BF16)
    h4 = matmul_res_ln(u, wl["w_down"], h3, wl["ln3_g"], wl["ln3_b"], alpha=alpha)
    return h4.reshape(B, T, D)


def kernel(x_prompt, x_sample, cache_k, cache_v, cache_kidx, state_hgrn, state_conv, cache_mem_k, cache_mem_v,
           page_table, mem_prompt, w_in, w_conv, hg_lb_logits, hg_norm, w_out, ln1_g, ln1_b,
           w_xq, w_xk, w_xv, w_xo, ln2_g, ln2_b, w_up, w_down, ln3_g, ln3_b):
    depth = w_in.shape[0]
    Bp, Tp, D = x_prompt.shape
    DB, Ts, _ = x_sample.shape
    alpha = (2 * depth) ** 0.25
    past = page_table.shape[1] * cache_k.shape[2]
    pos_p = jnp.arange(Tp)
    pos_s = past + jnp.arange(Ts)
    lower = _lower_bounds(hg_lb_logits)
    n_in = w_in.shape[-1]
    n_in_pad = -(-n_in // (5 * LANES)) * (5 * LANES)
    n_mem = mem_prompt.shape[1]
    xw = w_xk.shape[-1]
    hp, hs = x_prompt, x_sample
    outs = [[] for _ in range(12)]
    for l in range(depth):
        wl = dict(
            w_in=jnp.pad(w_in[l], ((0, 0), (0, n_in_pad - n_in))).astype(BF16),
            w_conv=w_conv[l], lower=lower[l], hg_norm=hg_norm[l],
            w_out=w_out[l].astype(BF16), ln1_g=ln1_g[l], ln1_b=ln1_b[l],
            w_xq=w_xq[l].astype(BF16), w_xo=w_xo[l].astype(BF16), ln2_g=ln2_g[l], ln2_b=ln2_b[l],
            w_up=w_up[l].astype(BF16), w_down=w_down[l].astype(BF16), ln3_g=ln3_g[l], ln3_b=ln3_b[l])
        conv0 = jnp.zeros((Bp, CONV_W - 1, D // 4), F32)
        hg0 = jnp.zeros((Bp, HG_HEADS, D // 4 // HG_HEADS, D // 4 // HG_HEADS), F32)
        mix, c_st, h_st, k, v, ki = _token_mixer(hp, pos_p, wl, conv0, hg0, dsa_prompt)
        w_kv = jnp.concatenate([w_xk[l], w_xv[l]], axis=1).astype(BF16)
        mkv = matmul(mem_prompt.reshape(Bp * n_mem, D), w_kv).reshape(Bp, n_mem, 2 * xw)
        mk, mv = mkv[..., :xw], mkv[..., xw:]
        hp = _layer_tail(hp, mix, mk, mv, wl, alpha)
        for lst, a in zip(outs[:3] + [outs[6], outs[8]], (k, v, ki, h_st, c_st)):
            lst.append(a)
        outs[10].append(mk.reshape(Bp, n_mem, X_HEADS, xw // X_HEADS))
        outs[11].append(mv.reshape(Bp, n_mem, X_HEADS, xw // X_HEADS))
        attend = functools.partial(dsa_sample, cache_k=cache_k, cache_v=cache_v, cache_kidx=cache_kidx,
                                   page_table=page_table, layer=l)
        mix, c_st, h_st, k, v, ki = _token_mixer(hs, pos_s, wl, state_conv[l], state_hgrn[l], attend)
        hs = _layer_tail(hs, mix, cache_mem_k[l].reshape(DB, n_mem, xw), cache_mem_v[l].reshape(DB, n_mem, xw),
                         wl, alpha)
        for lst, a in zip(outs[3:6] + [outs[7], outs[9]], (k, v, ki, h_st, c_st)):
            lst.append(a)
    st = lambda xs: jnp.stack(xs, axis=0)
    kp, vp, kip, ksl, vsl, kis, hgp, hgs, cvp, cvs, mkp, mvp = (st(o) for o in outs)
    return (hp, hs, kp, vp, kip, ksl, vsl, kis, hgp, hgs, cvp, cvs, mkp, mvp)
```
